```python
import math
import jax, jax.numpy as jnp
from jax import lax
import numpy as np

D_MODEL = 1024
BATCH = 8
SEQ = 2048
DEPTH = 4
DEC_BATCH = 128
DEC_SEQ = 4
PAST_LEN = 16384
PAGE_SIZE = 128

D_SSM = D_MODEL // 2
D_CONV = D_MODEL - D_SSM
SSM_GROUP = 16
N_SSM_GROUPS = D_SSM // SSM_GROUP
SSM_STATE = 64
CONV_WIDTH = 31
CONV_BUF = CONV_WIDTH - 1
D_FF = 4 * D_MODEL
D_IN = D_SSM + 2 * D_CONV
EPS = 1e-6
DT_MIN = 1e-3
DT_MAX = 1e-1

kernel_name = "hymba_s5_conformer_conv_step"


def rms_norm(x, g):
    xf = x.astype(jnp.float32)
    y = xf * lax.rsqrt(jnp.mean(xf * xf, axis=-1, keepdims=True) + EPS)
    return (y * g.astype(jnp.float32)).astype(x.dtype)


def layer_norm(x, g, b):
    xf = x.astype(jnp.float32)
    mu = jnp.mean(xf, axis=-1, keepdims=True)
    xc = xf - mu
    y = xc * lax.rsqrt(jnp.mean(xc * xc, axis=-1, keepdims=True) + EPS)
    return (y * g.astype(jnp.float32) + b.astype(jnp.float32)).astype(x.dtype)


def _lin_rec(e1, e2):
    a1, b1 = e1
    a2, b2 = e2
    return a2 * a1, a2 * b1 + b2


def s5_mixer(u, h0, a_re, a_im, log_dt, b_re, b_im, c_re, c_im, d_skip, w_glu, b_glu):
    n, l, _ = u.shape
    f32 = jnp.float32
    lam = lax.complex(a_re.astype(f32), a_im.astype(f32))
    dt = jnp.exp(log_dt.astype(f32))[:, None]
    lam_bar = jnp.exp(lam * dt)
    b = lax.complex(b_re.astype(f32), b_im.astype(f32))
    b_bar = ((lam_bar - 1.0) / lam)[..., None] * b
    uf = u.astype(f32)
    ug = uf.reshape(n, l, N_SSM_GROUPS, SSM_GROUP).astype(jnp.complex64)
    bu = jnp.einsum('gpc,nlgc->nlgp', b_bar, ug)
    a = jnp.broadcast_to(lam_bar, bu.shape)
    a_cum, h = lax.associative_scan(_lin_rec, (a, bu), axis=1)
    if h0 is not None:
        h = h + a_cum * h0[:, None]
    c = lax.complex(c_re.astype(f32), c_im.astype(f32))
    y = jnp.einsum('gcp,nlgp->nlgc', c, h).real.reshape(n, l, D_SSM)
    y = y + d_skip.astype(f32) * uf
    y = jax.nn.gelu(y)
    y = y * jax.nn.sigmoid(y @ w_glu.astype(f32) + b_glu.astype(f32))
    return y.astype(u.dtype), h[:, -1]


def conv_mixer(v, gate, buf, w_dw, b_dw, ln_g, ln_b):
    z = v * jax.nn.sigmoid(gate)
    zc = jnp.concatenate([buf.astype(z.dtype), z], axis=1)
    y = lax.conv_general_dilated(
        zc, w_dw[:, None, :].astype(z.dtype), window_strides=(1,), padding='VALID',
        dimension_numbers=('NWC', 'WIO', 'NWC'), feature_group_count=D_CONV)
    y = y + b_dw.astype(y.dtype)
    y = jax.nn.silu(layer_norm(y, ln_g, ln_b))
    return y, zc[:, -CONV_BUF:]


def setup_inputs(seed: int = 0) -> dict:
    key = jax.random.key(seed)
    ks = jax.random.split(key, 32)
    f32 = jnp.float32
    nrm = lambda k, s, sc: jax.random.normal(k, s, f32) * sc
    n_idx = jnp.arange(SSM_STATE, dtype=f32)
    a_re = -0.5 * jnp.exp(nrm(ks[5], (DEPTH, N_SSM_GROUPS, SSM_STATE), 0.01))
    a_im = math.pi * n_idx[None, None, :] + nrm(ks[6], (DEPTH, N_SSM_GROUPS, SSM_STATE), 0.01)
    log_dt = jax.random.uniform(ks[7], (DEPTH, N_SSM_GROUPS), f32,
                                math.log(DT_MIN), math.log(DT_MAX))
    return {
        "x_prompt": nrm(ks[0], (BATCH, SEQ, D_MODEL), 1.0),
        "x_sample": nrm(ks[1], (DEC_BATCH, DEC_SEQ, D_MODEL), 1.0),
        "state_ssm_re": nrm(ks[2], (DEPTH, DEC_BATCH, N_SSM_GROUPS, SSM_STATE), 0.1),
        "state_ssm_im": nrm(ks[3], (DEPTH, DEC_BATCH, N_SSM_GROUPS, SSM_STATE), 0.1),
        "state_conv": nrm(ks[4], (DEPTH, DEC_BATCH, CONV_BUF, D_CONV), 0.5),
        "norm_mix_g": 1.0 + nrm(ks[8], (DEPTH, D_MODEL), 0.01),
        "w_in": nrm(ks[9], (DEPTH, D_MODEL, D_IN), D_MODEL ** -0.5),
        "b_in": nrm(ks[10], (DEPTH, D_IN), 0.01),
        "ssm_a_re": a_re,
        "ssm_a_im": a_im,
        "ssm_log_dt": log_dt,
        "ssm_b_re": nrm(ks[11], (DEPTH, N_SSM_GROUPS, SSM_STATE, SSM_GROUP), (2 * SSM_GROUP) ** -0.5),
        "ssm_b_im": nrm(ks[12], (DEPTH, N_SSM_GROUPS, SSM_STATE, SSM_GROUP), (2 * SSM_GROUP) ** -0.5),
        "ssm_c_re": nrm(ks[13], (DEPTH, N_SSM_GROUPS, SSM_GROUP, SSM_STATE), SSM_STATE ** -0.5),
        "ssm_c_im": nrm(ks[14], (DEPTH, N_SSM_GROUPS, SSM_GROUP, SSM_STATE), SSM_STATE ** -0.5),
        "ssm_d": nrm(ks[15], (DEPTH, D_SSM), 1.0),
        "w_glu": nrm(ks[16], (DEPTH, D_SSM, D_SSM), D_SSM ** -0.5),
        "b_glu": nrm(ks[17], (DEPTH, D_SSM), 0.01),
        "conv_w": nrm(ks[18], (DEPTH, CONV_WIDTH, D_CONV), CONV_WIDTH ** -0.5),
        "conv_b": nrm(ks[19], (DEPTH, D_CONV), 0.01),
        "conv_ln_g": 1.0 + nrm(ks[20], (DEPTH, D_CONV), 0.01),
        "conv_ln_b": nrm(ks[21], (DEPTH, D_CONV), 0.01),
        "w_out": nrm(ks[22], (DEPTH, D_MODEL, D_MODEL), D_MODEL ** -0.5),
        "norm_mlp_g": 1.0 + nrm(ks[23], (DEPTH, D_MODEL), 0.01),
        "w_up": nrm(ks[24], (DEPTH, D_MODEL, D_FF), D_MODEL ** -0.5),
        "w_down": nrm(ks[25], (DEPTH, D_FF, D_MODEL), D_FF ** -0.5),
        "norm_f_g": 1.0 + nrm(ks[26], (D_MODEL,), 0.01),
    }


def reference(x_prompt, x_sample, state_ssm_re, state_ssm_im, state_conv,
              norm_mix_g, w_in, b_in, ssm_a_re, ssm_a_im, ssm_log_dt,
              ssm_b_re, ssm_b_im, ssm_c_re, ssm_c_im, ssm_d, w_glu, b_glu,
              conv_w, conv_b, conv_ln_g, conv_ln_b, w_out,
              norm_mlp_g, w_up, w_down, norm_f_g):

    def run_trunk(x, h0_re, h0_im, buf0):
        n = x.shape[0]
        h_re_out, h_im_out, buf_out = [], [], []
        for l in range(DEPTH):
            hn = rms_norm(x, norm_mix_g[l])
            proj = hn @ w_in[l] + b_in[l]
            u = proj[..., :D_SSM]
            v = proj[..., D_SSM:D_SSM + D_CONV]
            gate = proj[..., D_SSM + D_CONV:]
            if h0_re is None:
                h0 = None
                buf = jnp.zeros((n, CONV_BUF, D_CONV), x.dtype)
            else:
                h0 = lax.complex(h0_re[l].astype(jnp.float32), h0_im[l].astype(jnp.float32))
                buf = buf0[l]
            y_ssm, h_last = s5_mixer(u, h0, ssm_a_re[l], ssm_a_im[l], ssm_log_dt[l],
                                     ssm_b_re[l], ssm_b_im[l], ssm_c_re[l], ssm_c_im[l],
                                     ssm_d[l], w_glu[l], b_glu[l])
            y_conv, new_buf = conv_mixer(v, gate, buf, conv_w[l], conv_b[l],
                                         conv_ln_g[l], conv_ln_b[l])
            x = x + jnp.concatenate([y_ssm, y_conv.astype(y_ssm.dtype)], axis=-1) @ w_out[l]
            hn = rms_norm(x, norm_mlp_g[l])
            x = x + jnp.square(jax.nn.relu(hn @ w_up[l])) @ w_down[l]
            h_re_out.append(h_last.real)
            h_im_out.append(h_last.imag)
            buf_out.append(new_buf)
        y = rms_norm(x, norm_f_g)
        return y, jnp.stack(h_re_out), jnp.stack(h_im_out), jnp.stack(buf_out)

    y_prompt, ssm_re_p, ssm_im_p, conv_p = run_trunk(x_prompt, None, None, None)
    y_sample, ssm_re_s, ssm_im_s, conv_s = run_trunk(x_sample, state_ssm_re, state_ssm_im, state_conv)
    return (y_prompt, y_sample, ssm_re_p, ssm_im_p, conv_p, ssm_re_s, ssm_im_s, conv_s)
```

```python
import functools

import jax
import jax.numpy as jnp
from jax import lax
from jax.experimental import pallas as pl
from jax.experimental.pallas import tpu as pltpu

D_MODEL = 1024
DEPTH = 4
D_SSM = 512
D_CONV = 512
SSM_GROUP = 16
N_SSM_GROUPS = 32
SSM_STATE = 64
CONV_WIDTH = 31
CONV_BUF = CONV_WIDTH - 1
D_FF = 4 * D_MODEL
D_IN = D_SSM + 2 * D_CONV
EPS = 1e-6

SUBLANES = 8
LANES = 128
N_STATE = N_SSM_GROUPS * SSM_STATE
QUAD = 4 * SSM_STATE
N_QUADS = N_STATE // QUAD
H_WIDTH = 2 * N_STATE
VMEM_LIMIT = 56 * 1024 * 1024

f32 = jnp.float32
bf16 = jnp.bfloat16


def _rms_norm(x, g):
    return x * lax.rsqrt(jnp.mean(x * x, axis=-1, keepdims=True) + EPS) * g


def _dot(a, b):
    return jnp.dot(a, b, preferred_element_type=f32)


def _mixer_kernel(x_ref, h0_ref, buf0_ref, gmix_ref, win_ref, bin_ref, lam_ref, bblk_ref, cpair_ref,
                  d_ref, wglu_ref, bglu_ref, cw_ref, cb_ref, lng_ref, lnb_ref, wout_ref,
                  xo_ref, hlast_ref, tail_ref,
                  proj_ref, hbuf_ref, hc_ref, zc_ref, yc_ref, *, nb, tc, n_chunks):
    rows = nb * tc
    hist = CONV_BUF * nb
    nbt = nb // SUBLANES
    c = pl.program_id(1)

    @pl.when(c == 0)
    def _():
        hc_ref[...] = h0_ref[...]
        zc_ref[0:hist, :] = buf0_ref[...]

    x = x_ref[...]
    hn = _rms_norm(x, gmix_ref[...]).astype(bf16)
    proj_ref[...] = _dot(hn, win_ref[...]) + bin_ref[...]

    ub = proj_ref[:, 0:D_SSM].astype(bf16)
    for j in range(N_QUADS):
        slab = ub[:, LANES * (j // 2):LANES * (j // 2 + 1)]
        hbuf_ref[:, 2 * QUAD * j:2 * QUAD * (j + 1)] = _dot(slab, bblk_ref[j])

    pairs = [(j, a) for j in range(N_QUADS) for a in range(2)]
    for grp in range(len(pairs) // 4):
        cols = [2 * QUAD * j + LANES * a for (j, a) in pairs[4 * grp:4 * grp + 4]]
        scol = [QUAD * j + LANES * a for (j, a) in pairs[4 * grp:4 * grp + 4]]
        ar = [jnp.broadcast_to(lam_ref[0:1, s:s + LANES], (SUBLANES, LANES)) for s in scol]
        ai = [jnp.broadcast_to(lam_ref[1:2, s:s + LANES], (SUBLANES, LANES)) for s in scol]

        def bt_body(bt, carry, cols=cols, ar=ar, ai=ai):
            r0 = pl.multiple_of(bt * SUBLANES, SUBLANES)
            h = []
            for cr in cols:
                h.append(hc_ref[pl.ds(r0, SUBLANES), cr:cr + LANES])
                h.append(hc_ref[pl.ds(r0, SUBLANES), cr + QUAD:cr + QUAD + LANES])

            def t_body(t, h):
                row = pl.multiple_of(t * nb + r0, SUBLANES)
                new = []
                for i, cr in enumerate(cols):
                    hr, hi = h[2 * i], h[2 * i + 1]
                    br = hbuf_ref[pl.ds(row, SUBLANES), cr:cr + LANES]
                    bi = hbuf_ref[pl.ds(row, SUBLANES), cr + QUAD:cr + QUAD + LANES]
                    nr = ar[i] * hr - ai[i] * hi + br
                    ni = ar[i] * hi + ai[i] * hr + bi
                    hbuf_ref[pl.ds(row, SUBLANES), cr:cr + LANES] = nr
                    hbuf_ref[pl.ds(row, SUBLANES), cr + QUAD:cr + QUAD + LANES] = ni
                    new += [nr, ni]
                return tuple(new)

            h = lax.fori_loop(0, tc, t_body, tuple(h), unroll=min(tc, 4))
            for i, cr in enumerate(cols):
                hc_ref[pl.ds(r0, SUBLANES), cr:cr + LANES] = h[2 * i]
                hc_ref[pl.ds(r0, SUBLANES), cr + QUAD:cr + QUAD + LANES] = h[2 * i + 1]
            return carry

        lax.fori_loop(0, nbt, bt_body, 0)

    hlast_ref[...] = hc_ref[...]

    ys = []
    for jj in range(N_QUADS // 2):
        hq = hbuf_ref[:, 4 * QUAD * jj:4 * QUAD * (jj + 1)].astype(bf16)
        ys.append(_dot(hq, cpair_ref[jj]))
    y = jnp.concatenate(ys, axis=-1) + d_ref[...] * proj_ref[:, 0:D_SSM]
    y = jax.nn.gelu(y)
    y = y * jax.nn.sigmoid(_dot(y.astype(bf16), wglu_ref[...]) + bglu_ref[...])
    acc = x + _dot(y.astype(bf16), wout_ref[0:D_SSM, :])

    z = proj_ref[:, D_SSM:D_SSM + D_CONV] * jax.nn.sigmoid(proj_ref[:, D_SSM + D_CONV:D_IN])
    zc_ref[hist:hist + rows, :] = z
    tail_ref[...] = zc_ref[rows:rows + hist, :]

    for lc in range(D_CONV // LANES):
        lanes = slice(LANES * lc, LANES * (lc + 1))
        w = [jnp.broadcast_to(cw_ref[k:k + 1, lanes], (SUBLANES, LANES)) for k in range(CONV_WIDTH)]
        bias = jnp.broadcast_to(cb_ref[0:1, lanes], (SUBLANES, LANES))

        def conv_body(i, carry, lanes=lanes, w=w, bias=bias):
            tp = i // nbt
            bt = i - tp * nbt
            base = pl.multiple_of(2 * tp * nb + bt * SUBLANES, SUBLANES)
            xin = [zc_ref[pl.ds(base + m * nb, SUBLANES), lanes] for m in range(CONV_WIDTH + 1)]
            for o in range(2):
                parts = [None] * 4
                for k in range(CONV_WIDTH):
                    term = w[k] * xin[k + o]
                    parts[k % 4] = term if parts[k % 4] is None else parts[k % 4] + term
                out = (parts[0] + parts[1]) + (parts[2] + parts[3]) + bias
                yc_ref[pl.ds(base + o * nb, SUBLANES), lanes] = out
            return carry

        lax.fori_loop(0, (tc // 2) * nbt, conv_body, 0)

    if n_chunks > 1:
        zc_ref[0:hist, :] = zc_ref[rows:rows + hist, :]

    yc = yc_ref[...]
    mu = jnp.mean(yc, axis=-1, keepdims=True)
    xc = yc - mu
    yn = xc * lax.rsqrt(jnp.mean(xc * xc, axis=-1, keepdims=True) + EPS) * lng_ref[...] + lnb_ref[...]
    yconv = jax.nn.silu(yn)
    xo_ref[...] = acc + _dot(yconv.astype(bf16), wout_ref[D_SSM:D_MODEL, :])


def _const_spec(shape):
    nd = len(shape)
    return pl.BlockSpec(shape, lambda b, c, _nd=nd: (0,) * _nd)


def _mixer_call(x, h0, buf0, p, *, n_tiles, nb, tc, n_chunks):
    rows = nb * tc
    hist = CONV_BUF * nb
    kern = functools.partial(_mixer_kernel, nb=nb, tc=tc, n_chunks=n_chunks)
    weights = [p["gmix"], p["win"], p["bin"], p["lam"], p["bblk"], p["cpair"], p["d"], p["wglu"], p["bglu"],
               p["cw"], p["cb"], p["lng"], p["lnb"], p["wout"]]
    in_specs = [
        pl.BlockSpec((rows, D_MODEL), lambda b, c: (b * n_chunks + c, 0)),
        pl.BlockSpec((nb, H_WIDTH), lambda b, c: (b, 0)),
        pl.BlockSpec((hist, D_CONV), lambda b, c: (b, 0)),
    ] + [_const_spec(w.shape) for w in weights]
    out_specs = [
        pl.BlockSpec((rows, D_MODEL), lambda b, c: (b * n_chunks + c, 0)),
        pl.BlockSpec((nb, H_WIDTH), lambda b, c: (b, 0)),
        pl.BlockSpec((hist, D_CONV), lambda b, c: (b, 0)),
    ]
    out_shape = [
        jax.ShapeDtypeStruct(x.shape, f32),
        jax.ShapeDtypeStruct(h0.shape, f32),
        jax.ShapeDtypeStruct(buf0.shape, f32),
    ]
    scratch = [
        pltpu.VMEM((rows, D_IN), f32),
        pltpu.VMEM((rows, H_WIDTH), f32),
        pltpu.VMEM((nb, H_WIDTH), f32),
        pltpu.VMEM((hist + rows, D_CONV), f32),
        pltpu.VMEM((rows, D_CONV), f32),
    ]
    return pl.pallas_call(
        kern,
        grid=(n_tiles, n_chunks),
        in_specs=in_specs,
        out_specs=out_specs,
        out_shape=out_shape,
        scratch_shapes=scratch,
        compiler_params=pltpu.CompilerParams(
            dimension_semantics=("arbitrary", "arbitrary"), vmem_limit_bytes=VMEM_LIMIT),
        name="mixer",
    )(x, h0, buf0, *weights)


FF_BLOCK = 1024


def _mlp_kernel(x_ref, g_ref, wup_ref, wdown_ref, gf_ref, o_ref, *, final_norm):
    x = x_ref[...]
    hn = _rms_norm(x, g_ref[...]).astype(bf16)
    acc = x
    for k in range(D_FF // FF_BLOCK):
        h = _dot(hn, wup_ref[:, FF_BLOCK * k:FF_BLOCK * (k + 1)])
        h = jnp.square(jnp.maximum(h, 0.0)).astype(bf16)
        acc = acc + _dot(h, wdown_ref[FF_BLOCK * k:FF_BLOCK * (k + 1), :])
    if final_norm:
        acc = _rms_norm(acc, gf_ref[...])
    o_ref[...] = acc


def _mlp_call(x, g, wup, wdown, gf, *, final_norm, row_tile):
    n_rows = x.shape[0]
    kern = functools.partial(_mlp_kernel, final_norm=final_norm)
    return pl.pallas_call(
        kern,
        grid=(n_rows // row_tile,),
        in_specs=[
            pl.BlockSpec((row_tile, D_MODEL), lambda i: (i, 0)),
            pl.BlockSpec((1, D_MODEL), lambda i: (0, 0)),
            pl.BlockSpec((D_MODEL, D_FF), lambda i: (0, 0)),
            pl.BlockSpec((D_FF, D_MODEL), lambda i: (0, 0)),
            pl.BlockSpec((1, D_MODEL), lambda i: (0, 0)),
        ],
        out_specs=pl.BlockSpec((row_tile, D_MODEL), lambda i: (i, 0)),
        out_shape=jax.ShapeDtypeStruct(x.shape, f32),
        compiler_params=pltpu.CompilerParams(
            dimension_semantics=("arbitrary",), vmem_limit_bytes=VMEM_LIMIT),
        name="mlp",
    )(x, g, wup, wdown, gf)


def _block_diag4(m):
    nq, _, r, c = m.shape
    eye = jnp.eye(4, dtype=m.dtype)
    return jnp.einsum("jqrc,qs->jqrsc", m, eye).reshape(nq, 4 * r, 4 * c)


def _ssm_params(a_re, a_im, log_dt, b_re, b_im, c_re, c_im):
    lam = lax.complex(a_re.astype(f32), a_im.astype(f32))
    dt = jnp.exp(log_dt.astype(f32))[:, None]
    lam_bar = jnp.exp(lam * dt)
    b = lax.complex(b_re.astype(f32), b_im.astype(f32))
    b_bar = ((lam_bar - 1.0) / lam)[..., None] * b
    lam_rows = jnp.stack([lam_bar.real.reshape(N_STATE), lam_bar.imag.reshape(N_STATE)])

    def quads(m):
        return m.reshape(N_QUADS, 4, m.shape[1], m.shape[2])

    bt_re = _block_diag4(quads(jnp.swapaxes(b_bar.real, 1, 2)))
    bt_im = _block_diag4(quads(jnp.swapaxes(b_bar.imag, 1, 2)))
    blk = jnp.concatenate([bt_re, bt_im], axis=-1)
    zero = jnp.zeros_like(blk)
    odd = (jnp.arange(N_QUADS) % 2 == 1)[:, None, None]
    bblk = jnp.where(odd, jnp.concatenate([zero, blk], axis=1), jnp.concatenate([blk, zero], axis=1))

    ct_re = _block_diag4(quads(jnp.swapaxes(c_re.astype(f32), 1, 2)))
    ct_im = _block_diag4(quads(jnp.swapaxes(c_im.astype(f32), 1, 2)))
    cq = jnp.concatenate([ct_re, -ct_im], axis=1)
    cq = cq.reshape(N_QUADS // 2, 2, 2 * QUAD, 4 * SSM_GROUP)
    zc = jnp.zeros_like(cq[:, 0])
    cpair = jnp.concatenate([jnp.concatenate([cq[:, 0], zc], axis=-1),
                             jnp.concatenate([zc, cq[:, 1]], axis=-1)], axis=1)
    return lam_rows, bblk.astype(bf16), cpair.astype(bf16)


def _to_state_lanes(h_re, h_im):
    n = h_re.shape[0]
    re = h_re.reshape(n, N_QUADS, 1, QUAD)
    im = h_im.reshape(n, N_QUADS, 1, QUAD)
    return jnp.concatenate([re, im], axis=2).reshape(n, H_WIDTH)


def _from_state_lanes(h):
    n = h.shape[0]
    h = h.reshape(n, N_QUADS, 2, QUAD)
    return (h[:, :, 0].reshape(n, N_SSM_GROUPS, SSM_STATE), h[:, :, 1].reshape(n, N_SSM_GROUPS, SSM_STATE))


def _run_trunk(x, h0_re, h0_im, buf0, layers, norm_f_g, *, nb, tc, mlp_rows):
    n, l, _ = x.shape
    n_tiles = n // nb
    n_chunks = l // tc
    xt = x.reshape(n_tiles, nb, l, D_MODEL).transpose(0, 2, 1, 3).reshape(n * l, D_MODEL)
    h_re_out, h_im_out, buf_out = [], [], []
    for li, p in enumerate(layers):
        if h0_re is None:
            h0 = jnp.zeros((n, H_WIDTH), f32)
            b0 = jnp.zeros((n * CONV_BUF, D_CONV), f32)
        else:
            h0 = _to_state_lanes(h0_re[li], h0_im[li])
            b0 = buf0[li].reshape(n_tiles, nb, CONV_BUF, D_CONV).transpose(0, 2, 1, 3).reshape(n * CONV_BUF, D_CONV)
        xt, hl, tail = _mixer_call(xt, h0, b0, p, n_tiles=n_tiles, nb=nb, tc=tc, n_chunks=n_chunks)
        xt = _mlp_call(xt, p["gmlp"], p["wup"], p["wdown"], norm_f_g,
                       final_norm=(li == len(layers) - 1), row_tile=mlp_rows)
        hr, hi = _from_state_lanes(hl)
        h_re_out.append(hr)
        h_im_out.append(hi)
        buf_out.append(tail.reshape(n_tiles, CONV_BUF, nb, D_CONV).transpose(0, 2, 1, 3).reshape(n, CONV_BUF, D_CONV))
    y = xt.reshape(n_tiles, l, nb, D_MODEL).transpose(0, 2, 1, 3).reshape(n, l, D_MODEL)
    return y, jnp.stack(h_re_out), jnp.stack(h_im_out), jnp.stack(buf_out)


def kernel(x_prompt, x_sample, state_ssm_re, state_ssm_im, state_conv, norm_mix_g, w_in, b_in, ssm_a_re, ssm_a_im, ssm_log_dt, ssm_b_re, ssm_b_im, ssm_c_re, ssm_c_im, ssm_d, w_glu, b_glu, conv_w, conv_b, conv_ln_g, conv_ln_b, w_out, norm_mlp_g, w_up, w_down, norm_f_g):
    layers = []
    for l in range(DEPTH):
        lam_rows, bblk, cpair = _ssm_params(ssm_a_re[l], ssm_a_im[l], ssm_log_dt[l], ssm_b_re[l], ssm_b_im[l],
                                            ssm_c_re[l], ssm_c_im[l])
        layers.append(dict(
            gmix=norm_mix_g[l].reshape(1, D_MODEL), win=w_in[l].astype(bf16), bin=b_in[l].reshape(1, D_IN),
            lam=lam_rows, bblk=bblk, cpair=cpair, d=ssm_d[l].reshape(1, D_SSM),
            wglu=w_glu[l].astype(bf16), bglu=b_glu[l].reshape(1, D_SSM),
            cw=conv_w[l], cb=conv_b[l].reshape(1, D_CONV),
            lng=conv_ln_g[l].reshape(1, D_CONV), lnb=conv_ln_b[l].reshape(1, D_CONV),
            wout=w_out[l].astype(bf16), gmlp=norm_mlp_g[l].reshape(1, D_MODEL),
            wup=w_up[l].astype(bf16), wdown=w_down[l].astype(bf16)))
    gf = norm_f_g.reshape(1, D_MODEL)

    y_p, re_p, im_p, conv_p = _run_trunk(x_prompt, None, None, None, layers, gf, nb=8, tc=64, mlp_rows=512)
    y_s, re_s, im_s, conv_s = _run_trunk(x_sample, state_ssm_re, state_ssm_im, state_conv, layers, gf,
                                         nb=32, tc=x_sample.shape[1], mlp_rows=512)
    return (y_p, y_s, re_p, im_p, conv_p, re_s, im_s, conv_s)
```

```python
import functools

import jax
import jax.numpy as jnp
from jax import lax
from jax.experimental import pallas as pl
from jax.experimental.pallas import tpu as pltpu

D_MODEL = 1024
DEPTH = 4
D_SSM = 512
D_CONV = 512
SSM_GROUP = 16
N_SSM_GROUPS = 32
SSM_STATE = 64
CONV_WIDTH = 31
CONV_BUF = CONV_WIDTH - 1
D_FF = 4 * D_MODEL
D_IN = D_SSM + 2 * D_CONV
EPS = 1e-6
SCAN_CHAINS = 8
CONV_STEPS = 4

SUBLANES = 8
LANES = 128
N_LANE_COLS = D_CONV // LANES
N_STATE = N_SSM_GROUPS * SSM_STATE
QUAD = 4 * SSM_STATE
N_QUADS = N_STATE // QUAD
H_WIDTH = 2 * N_STATE
VMEM_LIMIT = 56 * 1024 * 1024

f32 = jnp.float32
bf16 = jnp.bfloat16


def _rms_norm(x, g):
    return x * lax.rsqrt(jnp.mean(x * x, axis=-1, keepdims=True) + EPS) * g


def _dot(a, b):
    return jnp.dot(a, b, preferred_element_type=f32)


def _mixer_kernel(x_ref, h0_ref, buf0_ref, gmix_ref, win_ref, bin_ref, lam_ref, bblk_ref, cpair_ref,
                  d_ref, wglu_ref, bglu_ref, cw_ref, cb_ref, lng_ref, lnb_ref, wout_ref,
                  xo_ref, hlast_ref, tail_ref,
                  proj_ref, hbuf_ref, hc_ref, zc_ref, yc_ref, wb_ref, *, nb, tc, n_chunks):
    rows = nb * tc
    hist = CONV_BUF * nb
    nbt = nb // SUBLANES
    c = pl.program_id(1)

    @pl.when(c == 0)
    def _():
        hc_ref[...] = h0_ref[...]
        for lc in range(N_LANE_COLS):
            lanes = slice(LANES * lc, LANES * (lc + 1))
            zc_ref[lc, 0:hist, :] = buf0_ref[:, lanes]
            for k in range(CONV_WIDTH):
                wb_ref[lc, SUBLANES * k:SUBLANES * (k + 1), :] = jnp.broadcast_to(
                    cw_ref[k:k + 1, lanes], (SUBLANES, LANES))

    x = x_ref[...]
    hn = _rms_norm(x, gmix_ref[...]).astype(bf16)
    proj_ref[...] = _dot(hn, win_ref[...]) + bin_ref[...]

    ub = proj_ref[:, 0:D_SSM].astype(bf16)
    for j in range(N_QUADS):
        slab = ub[:, LANES * (j // 2):LANES * (j // 2 + 1)]
        hbuf_ref[:, 2 * QUAD * j:2 * QUAD * (j + 1)] = _dot(slab, bblk_ref[j])

    pairs = [(j, a) for j in range(N_QUADS) for a in range(2)]
    for grp in range(len(pairs) // SCAN_CHAINS):
        grp_pairs = pairs[SCAN_CHAINS * grp:SCAN_CHAINS * (grp + 1)]
        cols = [2 * QUAD * j + LANES * a for (j, a) in grp_pairs]
        scol = [QUAD * j + LANES * a for (j, a) in grp_pairs]
        ar = [jnp.broadcast_to(lam_ref[0:1, s:s + LANES], (SUBLANES, LANES)) for s in scol]
        ai = [jnp.broadcast_to(lam_ref[1:2, s:s + LANES], (SUBLANES, LANES)) for s in scol]

        def bt_body(bt, carry, cols=cols, ar=ar, ai=ai):
            r0 = pl.multiple_of(bt * SUBLANES, SUBLANES)
            h = []
            for cr in cols:
                h.append(hc_ref[pl.ds(r0, SUBLANES), cr:cr + LANES])
                h.append(hc_ref[pl.ds(r0, SUBLANES), cr + QUAD:cr + QUAD + LANES])

            def t_body(t, h):
                row = pl.multiple_of(t * nb + r0, SUBLANES)
                new = []
                for i, cr in enumerate(cols):
                    hr, hi = h[2 * i], h[2 * i + 1]
                    br = hbuf_ref[pl.ds(row, SUBLANES), cr:cr + LANES]
                    bi = hbuf_ref[pl.ds(row, SUBLANES), cr + QUAD:cr + QUAD + LANES]
                    nr = ar[i] * hr - ai[i] * hi + br
                    ni = ar[i] * hi + ai[i] * hr + bi
                    hbuf_ref[pl.ds(row, SUBLANES), cr:cr + LANES] = nr
                    hbuf_ref[pl.ds(row, SUBLANES), cr + QUAD:cr + QUAD + LANES] = ni
                    new += [nr, ni]
                return tuple(new)

            h = lax.fori_loop(0, tc, t_body, tuple(h), unroll=min(tc, 8))
            for i, cr in enumerate(cols):
                hc_ref[pl.ds(r0, SUBLANES), cr:cr + LANES] = h[2 * i]
                hc_ref[pl.ds(r0, SUBLANES), cr + QUAD:cr + QUAD + LANES] = h[2 * i + 1]
            return carry

        lax.fori_loop(0, nbt, bt_body, 0)

    hlast_ref[...] = hc_ref[...]

    ys = []
    for jj in range(N_QUADS // 2):
        hq = hbuf_ref[:, 4 * QUAD * jj:4 * QUAD * (jj + 1)].astype(bf16)
        ys.append(_dot(hq, cpair_ref[jj]))
    y = jnp.concatenate(ys, axis=-1) + d_ref[...] * proj_ref[:, 0:D_SSM]
    y = jax.nn.gelu(y)
    y = y * jax.nn.sigmoid(_dot(y.astype(bf16), wglu_ref[...]) + bglu_ref[...])
    acc = x + _dot(y.astype(bf16), wout_ref[0:D_SSM, :])

    z = proj_ref[:, D_SSM:D_SSM + D_CONV] * jax.nn.sigmoid(proj_ref[:, D_SSM + D_CONV:D_IN])
    for lc in range(N_LANE_COLS):
        lanes = slice(LANES * lc, LANES * (lc + 1))
        zc_ref[lc, hist:hist + rows, :] = z[:, lanes]
        tail_ref[:, lanes] = zc_ref[lc, rows:rows + hist, :]
        bias = jnp.broadcast_to(cb_ref[0:1, lanes], (SUBLANES, LANES))

        def conv_body(i, carry, lc=lc, bias=bias):
            tp = i // nbt
            bt = i - tp * nbt
            base = pl.multiple_of(CONV_STEPS * tp * nb + bt * SUBLANES, SUBLANES)
            win = zc_ref.at[lc, pl.ds(base, (CONV_WIDTH + CONV_STEPS - 2) * nb + SUBLANES), :]
            dst = yc_ref.at[lc, pl.ds(base, (CONV_STEPS - 1) * nb + SUBLANES), :]
            xin = [win[m * nb:m * nb + SUBLANES, :] for m in range(CONV_WIDTH + CONV_STEPS - 1)]
            parts = [[None, None] for _ in range(CONV_STEPS)]
            for k in range(CONV_WIDTH):
                wk = wb_ref[lc, SUBLANES * k:SUBLANES * (k + 1), :]
                for o in range(CONV_STEPS):
                    term = wk * xin[k + o]
                    parts[o][k % 2] = term if parts[o][k % 2] is None else parts[o][k % 2] + term
            for o in range(CONV_STEPS):
                dst[o * nb:o * nb + SUBLANES, :] = (parts[o][0] + parts[o][1]) + bias
            return carry

        lax.fori_loop(0, (tc // CONV_STEPS) * nbt, conv_body, 0)
        if n_chunks > 1:
            zc_ref[lc, 0:hist, :] = zc_ref[lc, rows:rows + hist, :]

    yc = jnp.concatenate([yc_ref[lc] for lc in range(N_LANE_COLS)], axis=-1)
    mu = jnp.mean(yc, axis=-1, keepdims=True)
    xc = yc - mu
    yn = xc * lax.rsqrt(jnp.mean(xc * xc, axis=-1, keepdims=True) + EPS) * lng_ref[...] + lnb_ref[...]
    yconv = jax.nn.silu(yn)
    xo_ref[...] = acc + _dot(yconv.astype(bf16), wout_ref[D_SSM:D_MODEL, :])


def _const_spec(shape):
    nd = len(shape)
    return pl.BlockSpec(shape, lambda b, c, _nd=nd: (0,) * _nd)


def _mixer_call(x, h0, buf0, p, *, n_tiles, nb, tc, n_chunks):
    rows = nb * tc
    hist = CONV_BUF * nb
    kern = functools.partial(_mixer_kernel, nb=nb, tc=tc, n_chunks=n_chunks)
    weights = [p["gmix"], p["win"], p["bin"], p["lam"], p["bblk"], p["cpair"], p["d"], p["wglu"], p["bglu"],
               p["cw"], p["cb"], p["lng"], p["lnb"], p["wout"]]
    in_specs = [
        pl.BlockSpec((rows, D_MODEL), lambda b, c: (b * n_chunks + c, 0)),
        pl.BlockSpec((nb, H_WIDTH), lambda b, c: (b, 0)),
        pl.BlockSpec((hist, D_CONV), lambda b, c: (b, 0)),
    ] + [_const_spec(w.shape) for w in weights]
    out_specs = [
        pl.BlockSpec((rows, D_MODEL), lambda b, c: (b * n_chunks + c, 0)),
        pl.BlockSpec((nb, H_WIDTH), lambda b, c: (b, 0)),
        pl.BlockSpec((hist, D_CONV), lambda b, c: (b, 0)),
    ]
    out_shape = [
        jax.ShapeDtypeStruct(x.shape, f32),
        jax.ShapeDtypeStruct(h0.shape, f32),
        jax.ShapeDtypeStruct(buf0.shape, f32),
    ]
    scratch = [
        pltpu.VMEM((rows, D_IN), f32),
        pltpu.VMEM((rows, H_WIDTH), f32),
        pltpu.VMEM((nb, H_WIDTH), f32),
        pltpu.VMEM((N_LANE_COLS, hist + rows, LANES), f32),
        pltpu.VMEM((N_LANE_COLS, rows, LANES), f32),
        pltpu.VMEM((N_LANE_COLS, SUBLANES * CONV_WIDTH, LANES), f32),
    ]
    return pl.pallas_call(
        kern,
        grid=(n_tiles, n_chunks),
        in_specs=in_specs,
        out_specs=out_specs,
        out_shape=out_shape,
        scratch_shapes=scratch,
        compiler_params=pltpu.CompilerParams(
            dimension_semantics=("arbitrary", "arbitrary"), vmem_limit_bytes=VMEM_LIMIT),
        name="mixer",
    )(x, h0, buf0, *weights)


FF_BLOCK = 1024


def _mlp_kernel(x_ref, g_ref, wup_ref, wdown_ref, gf_ref, o_ref, *, final_norm):
    x = x_ref[...]
    hn = _rms_norm(x, g_ref[...]).astype(bf16)
    acc = x
    for k in range(D_FF // FF_BLOCK):
        h = _dot(hn, wup_ref[:, FF_BLOCK * k:FF_BLOCK * (k + 1)])
        h = jnp.square(jnp.maximum(h, 0.0)).astype(bf16)
        acc = acc + _dot(h, wdown_ref[FF_BLOCK * k:FF_BLOCK * (k + 1), :])
    if final_norm:
        acc = _rms_norm(acc, gf_ref[...])
    o_ref[...] = acc


def _mlp_call(x, g, wup, wdown, gf, *, final_norm, row_tile):
    n_rows = x.shape[0]
    kern = functools.partial(_mlp_kernel, final_norm=final_norm)
    return pl.pallas_call(
        kern,
        grid=(n_rows // row_tile,),
        in_specs=[
            pl.BlockSpec((row_tile, D_MODEL), lambda i: (i, 0)),
            pl.BlockSpec((1, D_MODEL), lambda i: (0, 0)),
            pl.BlockSpec((D_MODEL, D_FF), lambda i: (0, 0)),
            pl.BlockSpec((D_FF, D_MODEL), lambda i: (0, 0)),
            pl.BlockSpec((1, D_MODEL), lambda i: (0, 0)),
        ],
        out_specs=pl.BlockSpec((row_tile, D_MODEL), lambda i: (i, 0)),
        out_shape=jax.ShapeDtypeStruct(x.shape, f32),
        compiler_params=pltpu.CompilerParams(
            dimension_semantics=("arbitrary",), vmem_limit_bytes=VMEM_LIMIT),
        name="mlp",
    )(x, g, wup, wdown, gf)


def _block_diag4(m):
    nq, _, r, c = m.shape
    eye = jnp.eye(4, dtype=m.dtype)
    return jnp.einsum("jqrc,qs->jqrsc", m, eye).reshape(nq, 4 * r, 4 * c)


def _ssm_params(a_re, a_im, log_dt, b_re, b_im, c_re, c_im):
    lam = lax.complex(a_re.astype(f32), a_im.astype(f32))
    dt = jnp.exp(log_dt.astype(f32))[:, None]
    lam_bar = jnp.exp(lam * dt)
    b = lax.complex(b_re.astype(f32), b_im.astype(f32))
    b_bar = ((lam_bar - 1.0) / lam)[..., None] * b
    lam_rows = jnp.stack([lam_bar.real.reshape(N_STATE), lam_bar.imag.reshape(N_STATE)])

    def quads(m):
        return m.reshape(N_QUADS, 4, m.shape[1], m.shape[2])

    bt_re = _block_diag4(quads(jnp.swapaxes(b_bar.real, 1, 2)))
    bt_im = _block_diag4(quads(jnp.swapaxes(b_bar.imag, 1, 2)))
    blk = jnp.concatenate([bt_re, bt_im], axis=-1)
    zero = jnp.zeros_like(blk)
    odd = (jnp.arange(N_QUADS) % 2 == 1)[:, None, None]
    bblk = jnp.where(odd, jnp.concatenate([zero, blk], axis=1), jnp.concatenate([blk, zero], axis=1))

    ct_re = _block_diag4(quads(jnp.swapaxes(c_re.astype(f32), 1, 2)))
    ct_im = _block_diag4(quads(jnp.swapaxes(c_im.astype(f32), 1, 2)))
    cq = jnp.concatenate([ct_re, -ct_im], axis=1)
    cq = cq.reshape(N_QUADS // 2, 2, 2 * QUAD, 4 * SSM_GROUP)
    zc = jnp.zeros_like(cq[:, 0])
    cpair = jnp.concatenate([jnp.concatenate([cq[:, 0], zc], axis=-1),
                             jnp.concatenate([zc, cq[:, 1]], axis=-1)], axis=1)
    return lam_rows, bblk.astype(bf16), cpair.astype(bf16)


def _to_state_lanes(h_re, h_im):
    n = h_re.shape[0]
    re = h_re.reshape(n, N_QUADS, 1, QUAD)
    im = h_im.reshape(n, N_QUADS, 1, QUAD)
    return jnp.concatenate([re, im], axis=2).reshape(n, H_WIDTH)


def _from_state_lanes(h):
    n = h.shape[0]
    h = h.reshape(n, N_QUADS, 2, QUAD)
    return (h[:, :, 0].reshape(n, N_SSM_GROUPS, SSM_STATE), h[:, :, 1].reshape(n, N_SSM_GROUPS, SSM_STATE))


def _run_trunk(x, h0_re, h0_im, buf0, layers, norm_f_g, *, nb, tc, mlp_rows):
    n, l, _ = x.shape
    n_tiles = n // nb
    n_chunks = l // tc
    xt = x.reshape(n_tiles, nb, l, D_MODEL).transpose(0, 2, 1, 3).reshape(n * l, D_MODEL)
    h_re_out, h_im_out, buf_out = [], [], []
    for li, p in enumerate(layers):
        if h0_re is None:
            h0 = jnp.zeros((n, H_WIDTH), f32)
            b0 = jnp.zeros((n * CONV_BUF, D_CONV), f32)
        else:
            h0 = _to_state_lanes(h0_re[li], h0_im[li])
            b0 = buf0[li].reshape(n_tiles, nb, CONV_BUF, D_CONV).transpose(0, 2, 1, 3).reshape(n * CONV_BUF, D_CONV)
        xt, hl, tail = _mixer_call(xt, h0, b0, p, n_tiles=n_tiles, nb=nb, tc=tc, n_chunks=n_chunks)
        xt = _mlp_call(xt, p["gmlp"], p["wup"], p["wdown"], norm_f_g,
                       final_norm=(li == len(layers) - 1), row_tile=mlp_rows)
        hr, hi = _from_state_lanes(hl)
        h_re_out.append(hr)
        h_im_out.append(hi)
        buf_out.append(tail.reshape(n_tiles, CONV_BUF, nb, D_CONV).transpose(0, 2, 1, 3).reshape(n, CONV_BUF, D_CONV))
    y = xt.reshape(n_tiles, l, nb, D_MODEL).transpose(0, 2, 1, 3).reshape(n, l, D_MODEL)
    return y, jnp.stack(h_re_out), jnp.stack(h_im_out), jnp.stack(buf_out)


def kernel(x_prompt, x_sample, state_ssm_re, state_ssm_im, state_conv, norm_mix_g, w_in, b_in, ssm_a_re, ssm_a_im, ssm_log_dt, ssm_b_re, ssm_b_im, ssm_c_re, ssm_c_im, ssm_d, w_glu, b_glu, conv_w, conv_b, conv_ln_g, conv_ln_b, w_out, norm_mlp_g, w_up, w_down, norm_f_g):
    layers = []
    for l in range(DEPTH):
        lam_rows, bblk, cpair = _ssm_params(ssm_a_re[l], ssm_a_im[l], ssm_log_dt[l], ssm_b_re[l], ssm_b_im[l],
                                            ssm_c_re[l], ssm_c_im[l])
        layers.append(dict(
            gmix=norm_mix_g[l].reshape(1, D_MODEL), win=w_in[l].astype(bf16), bin=b_in[l].reshape(1, D_IN),
            lam=lam_rows, bblk=bblk, cpair=cpair, d=ssm_d[l].reshape(1, D_SSM),
            wglu=w_glu[l].astype(bf16), bglu=b_glu[l].reshape(1, D_SSM),
            cw=conv_w[l], cb=conv_b[l].reshape(1, D_CONV),
            lng=conv_ln_g[l].reshape(1, D_CONV), lnb=conv_ln_b[l].reshape(1, D_CONV),
            wout=w_out[l].astype(bf16), gmlp=norm_mlp_g[l].reshape(1, D_MODEL),
            wup=w_up[l].astype(bf16), wdown=w_down[l].astype(bf16)))
    gf = norm_f_g.reshape(1, D_MODEL)

    y_p, re_p, im_p, conv_p = _run_trunk(x_prompt, None, None, None, layers, gf, nb=8, tc=64, mlp_rows=512)
    y_s, re_s, im_s, conv_s = _run_trunk(x_sample, state_ssm_re, state_ssm_im, state_conv, layers, gf,
                                         nb=32, tc=x_sample.shape[1], mlp_rows=512)
    return (y_p, y_s, re_p, im_p, conv_p, re_s, im_s, conv_s)
```
